```python
import jax, jax.numpy as jnp
from jax import lax
import numpy as np

D_MODEL = 1024
BATCH = 8
SEQ = 8192
DEPTH = 1

HEAD_DIM = 64
D_MIX = D_MODEL
D_CONV = D_MIX // 2
D_ATTN = D_MIX - D_CONV
N_HEADS = D_ATTN // HEAD_DIM
CONV_WIDTH = 3
DILATED_BRANCHES = ((128, 1), (512, 4), (2048, 16))
BLOCK = 128
D_FF = ((-(-8 * D_MODEL // 3) + 255) // 256) * 256
D_IN = 3 * D_CONV + 3 * D_ATTN
EPS = 1e-6

kernel_name = "hybrid_shortconv_dilated_swa_swiglu"


def rms_norm(x, g):
    xf = x.astype(jnp.float32)
    y = xf * lax.rsqrt(jnp.mean(xf * xf, axis=-1, keepdims=True) + EPS)
    return (y * g.astype(jnp.float32)).astype(x.dtype)


def short_conv(u, w):
    return lax.conv_general_dilated(
        u, w[:, None, :].astype(u.dtype), window_strides=(1,),
        padding=[(CONV_WIDTH - 1, 0)], dimension_numbers=("NWC", "WIO", "NWC"),
        feature_group_count=u.shape[-1])


def dilated_branch(q, k, v, window, dil):
    B, S, H, dh = q.shape
    M = S // dil
    L = window // dil
    nb = -(-M // BLOCK)
    Mp = nb * BLOCK
    pad = Mp - M

    def to_sub(t):
        return t.reshape(B, M, dil, H, dh).transpose(0, 2, 1, 3, 4).reshape(B * dil, M, H, dh)

    qs, ks, vs = to_sub(q), to_sub(k), to_sub(v)
    Bd = B * dil
    qb = jnp.pad(qs, ((0, 0), (0, pad), (0, 0), (0, 0))).reshape(Bd, nb, BLOCK, H, dh)

    def band(t):
        tp = jnp.pad(t, ((0, 0), (BLOCK, pad), (0, 0), (0, 0)))
        prev = tp[:, :Mp].reshape(Bd, nb, BLOCK, H, dh)
        cur = tp[:, BLOCK:].reshape(Bd, nb, BLOCK, H, dh)
        return jnp.concatenate([prev, cur], axis=2)

    kw, vw = band(ks), band(vs)
    s = jnp.einsum("bnqhd,bnkhd->bnhqk", qb, kw, preferred_element_type=jnp.float32)

    i = jnp.arange(BLOCK)[:, None]
    j = jnp.arange(2 * BLOCK)[None, :]
    dist = BLOCK + i - j
    kpos = (jnp.arange(nb)[:, None, None] - 1) * BLOCK + j[None]
    valid = ((dist >= 0) & (dist <= L))[None] & (kpos >= 0)
    s = jnp.where(valid[None, :, None], s, -jnp.inf)

    m = jnp.max(s, axis=-1, keepdims=True)
    e = jnp.exp(s - m)
    den = jnp.sum(e, axis=-1, keepdims=True)
    p = e / den
    lse = (m + jnp.log(den))[..., 0]
    o = jnp.einsum("bnhqk,bnkhd->bnqhd", p.astype(vw.dtype), vw)

    o = o.reshape(Bd, Mp, H, dh)[:, :M]
    lse = lse.transpose(0, 1, 3, 2).reshape(Bd, Mp, H)[:, :M]
    o = o.reshape(B, dil, M, H, dh).transpose(0, 2, 1, 3, 4).reshape(B, S, H, dh)
    lse = lse.reshape(B, dil, M, H).transpose(0, 2, 1, 3).reshape(B, S, H)
    return o, lse


def dilated_mixture(q, k, v):
    outs, lses = [], []
    for window, dil in DILATED_BRANCHES:
        o, lse = dilated_branch(q, k, v, window, dil)
        outs.append(o)
        lses.append(lse)
    w = jax.nn.softmax(jnp.stack(lses, axis=0), axis=0)
    o = jnp.sum(w[..., None] * jnp.stack(outs, axis=0).astype(jnp.float32), axis=0)
    return o.astype(q.dtype)


def setup_inputs(seed: int = 0) -> dict:
    key = jax.random.key(seed)
    ks = jax.random.split(key, 13)
    f32 = jnp.float32

    def gain(k_, n):
        return 1.0 + 0.05 * jax.random.normal(k_, (DEPTH, n), f32)

    return {
        "x": jax.random.normal(ks[0], (BATCH, SEQ, D_MODEL), f32),
        "g_mix": gain(ks[1], D_MODEL),
        "w_in": jax.random.normal(ks[2], (DEPTH, D_MODEL, D_IN), f32) * D_MODEL ** -0.5,
        "conv_w": jax.random.normal(ks[3], (DEPTH, CONV_WIDTH, D_CONV), f32) * CONV_WIDTH ** -0.5,
        "g_q": gain(ks[4], HEAD_DIM),
        "g_k": gain(ks[5], HEAD_DIM),
        "g_conv_out": gain(ks[6], D_CONV),
        "g_attn_out": gain(ks[7], D_ATTN),
        "w_out": jax.random.normal(ks[8], (DEPTH, D_MIX, D_MODEL), f32) * D_MIX ** -0.5,
        "g_ffn": gain(ks[9], D_MODEL),
        "w_gate": jax.random.normal(ks[10], (DEPTH, D_MODEL, D_FF), f32) * D_MODEL ** -0.5,
        "w_up": jax.random.normal(ks[11], (DEPTH, D_MODEL, D_FF), f32) * D_MODEL ** -0.5,
        "w_down": jax.random.normal(ks[12], (DEPTH, D_FF, D_MODEL), f32) * D_FF ** -0.5,
    }


def reference(x, g_mix, w_in, conv_w, g_q, g_k, g_conv_out, g_attn_out, w_out,
              g_ffn, w_gate, w_up, w_down):
    B, S, _ = x.shape
    splits = [D_CONV, 2 * D_CONV, 3 * D_CONV, 3 * D_CONV + D_ATTN, 3 * D_CONV + 2 * D_ATTN]
    for l in range(DEPTH):
        h = rms_norm(x, g_mix[l])
        z = h @ w_in[l]
        u, gb, gc, q, k, v = jnp.split(z, splits, axis=-1)

        y_conv = gb * short_conv(gc * u, conv_w[l])

        q = rms_norm(q.reshape(B, S, N_HEADS, HEAD_DIM), g_q[l]) * (HEAD_DIM ** -0.5)
        k = rms_norm(k.reshape(B, S, N_HEADS, HEAD_DIM), g_k[l])
        v = v.reshape(B, S, N_HEADS, HEAD_DIM)
        y_attn = dilated_mixture(q, k, v).reshape(B, S, D_ATTN)

        mix = jnp.concatenate([rms_norm(y_conv, g_conv_out[l]),
                               rms_norm(y_attn, g_attn_out[l])], axis=-1)
        x = x + mix @ w_out[l]

        h = rms_norm(x, g_ffn[l])
        x = x + (jax.nn.silu(h @ w_gate[l]) * (h @ w_up[l])) @ w_down[l]
    return x
```

```python
import functools

import jax
import jax.numpy as jnp
from jax import lax
from jax.experimental import pallas as pl
from jax.experimental.pallas import tpu as pltpu

D_MODEL = 1024
HEAD_DIM = 64
D_CONV = 512
D_ATTN = 512
N_HEADS = D_ATTN // HEAD_DIM
CONV_WIDTH = 3
DILATED_BRANCHES = ((128, 1), (512, 4), (2048, 16))
BLOCK = 128
D_FF = 2816
EPS = 1e-6

LANES = 128
SUBLANES = 8
PAIR = 2 * HEAD_DIM
N_PAIRS = D_ATTN // PAIR
ROW_TILE = 512
ATTN_ROWS = 512
FF_CHUNK = 1408
VMEM_LIMIT = 56 * 1024 * 1024

assert PAIR == LANES and D_FF % FF_CHUNK == 0 and FF_CHUNK % LANES == 0

_F32 = jnp.float32
_BF16 = jnp.bfloat16


def _rms(x, g):
    ms = jnp.mean(x * x, axis=-1, keepdims=True)
    return x * lax.rsqrt(ms + EPS) * g


def _head_rms(z, g_tiled):
    lo = lax.broadcasted_iota(jnp.int32, (1, PAIR), 1) < HEAD_DIM
    outs = []
    for p in range(N_PAIRS):
        z2 = z[:, p * PAIR:(p + 1) * PAIR]
        sq = z2 * z2
        s_lo = jnp.sum(jnp.where(lo, sq, 0.0), axis=-1, keepdims=True)
        s_hi = jnp.sum(jnp.where(lo, 0.0, sq), axis=-1, keepdims=True)
        ms = jnp.where(lo, s_lo, s_hi) * (1.0 / HEAD_DIM)
        outs.append(z2 * lax.rsqrt(ms + EPS))
    return jnp.concatenate(outs, axis=-1) * g_tiled


def _in_proj_kernel(x_ref, gmix_ref, w_ref, cw_ref, gq_ref, gk_ref, gco_ref,
                    mixc_ref, q_ref, k_ref, v_ref, cbuf):
    s_idx = pl.program_id(1)
    tm = x_ref.shape[1]
    h = _rms(x_ref[0], gmix_ref[...]).astype(_BF16)

    def proj(c):
        return jnp.dot(h, w_ref[:, c * D_CONV:(c + 1) * D_CONV], preferred_element_type=_F32)

    @pl.when(s_idx == 0)
    def _():
        cbuf[0:SUBLANES, :] = jnp.zeros((SUBLANES, D_CONV), _F32)

    @pl.when(s_idx > 0)
    def _():
        cbuf[0:SUBLANES, :] = cbuf[tm:tm + SUBLANES, :]

    cu = proj(2) * proj(0)
    cbuf[SUBLANES:SUBLANES + tm, :] = cu
    c1 = cbuf[SUBLANES - 1:SUBLANES - 1 + tm, :]
    c2 = cbuf[SUBLANES - 2:SUBLANES - 2 + tm, :]
    cw = cw_ref[...]
    y = proj(1) * (cw[0:1, :] * c2 + cw[1:2, :] * c1 + cw[2:3, :] * cu)
    mixc_ref[0] = _rms(y, gco_ref[...]).astype(_BF16)

    q_ref[0] = _head_rms(proj(3), gq_ref[...]).astype(_BF16)
    k_ref[0] = _head_rms(proj(4), gk_ref[...]).astype(_BF16)
    v_ref[0] = proj(5).astype(_BF16)


def _in_proj(x, g_mix, w_in, conv_w, gq_t, gk_t, g_conv_out):
    b, s, _ = x.shape
    tm = ROW_TILE
    row = lambda bi, si: (bi, si, 0)
    const = lambda bi, si: (0, 0)
    out_sds = jax.ShapeDtypeStruct((b, s, D_CONV), _BF16)
    return pl.pallas_call(
        _in_proj_kernel,
        grid=(b, s // tm),
        in_specs=[
            pl.BlockSpec((1, tm, D_MODEL), row),
            pl.BlockSpec((1, D_MODEL), const),
            pl.BlockSpec(w_in.shape, const),
            pl.BlockSpec(conv_w.shape, const),
            pl.BlockSpec((1, D_ATTN), const),
            pl.BlockSpec((1, D_ATTN), const),
            pl.BlockSpec((1, D_CONV), const),
        ],
        out_specs=[pl.BlockSpec((1, tm, D_CONV), row)] * 4,
        out_shape=[out_sds] * 4,
        scratch_shapes=[pltpu.VMEM((SUBLANES + tm, D_CONV), _F32)],
        compiler_params=pltpu.CompilerParams(
            dimension_semantics=("arbitrary", "arbitrary"), vmem_limit_bytes=VMEM_LIMIT),
        name="in_proj",
    )(x, g_mix, w_in, conv_w, gq_t, gk_t, g_conv_out)


def _attn_kernel(bias_ref, q_ref, kc_ref, kp_ref, vc_ref, vp_ref, o_ref, st_ref, kbuf, vbuf):
    i = pl.program_id(2)
    tq = q_ref.shape[1]
    kbuf[0:BLOCK, :] = kp_ref[0]
    kbuf[BLOCK:BLOCK + tq, :] = kc_ref[0]
    vbuf[0:BLOCK, :] = vp_ref[0]
    vbuf[BLOCK:BLOCK + tq, :] = vc_ref[0]
    lane = lax.broadcasted_iota(jnp.int32, (1, LANES), 1)
    lo = lane < HEAD_DIM

    def q_block(j, carry):
        r0 = pl.multiple_of(j * BLOCK, BLOCK)
        first = jnp.logical_and(i == 0, j == 0).astype(jnp.int32)
        bias = bias_ref[first]
        stats = jnp.zeros((BLOCK, LANES), _F32)
        for p in range(N_PAIRS):
            cols = slice(p * PAIR, (p + 1) * PAIR)
            q2 = q_ref[0, pl.ds(r0, BLOCK), cols]
            k2 = kbuf[pl.ds(r0, 2 * BLOCK), cols]
            v2 = vbuf[pl.ds(r0, 2 * BLOCK), cols]
            outs = []
            for hh in range(2):
                qm = jnp.where(lo if hh == 0 else jnp.logical_not(lo), q2, jnp.zeros_like(q2))
                s = lax.dot_general(qm, k2, (((1,), (1,)), ((), ())),
                                    preferred_element_type=_F32) + bias
                m = jnp.max(s, axis=-1, keepdims=True)
                e = jnp.exp(s - m)
                den = jnp.sum(e, axis=-1, keepdims=True)
                pv = jnp.dot(e.astype(_BF16), v2, preferred_element_type=_F32)
                outs.append(pv * (1.0 / den))
                stats = jnp.where(lane == 2 * p + hh, m + jnp.log(den), stats)
            o_ref[0, pl.ds(r0, BLOCK), cols] = jnp.where(lo, outs[0], outs[1]).astype(_BF16)
        st_ref[0, pl.ds(r0, BLOCK), :] = stats
        return carry

    lax.fori_loop(0, tq // BLOCK, q_block, 0)


def _band_bias():
    i = jnp.arange(BLOCK)[:, None]
    j = jnp.arange(2 * BLOCK)[None, :]
    dist = BLOCK + i - j
    ok = (dist >= 0) & (dist <= BLOCK)
    ok_first = ok & (j >= BLOCK)
    neg = jnp.float32(-jnp.inf)
    return jnp.stack([jnp.where(ok, 0.0, neg), jnp.where(ok_first, 0.0, neg)]).astype(_F32)


def _attn_branch(bias, q, k, v, dil):
    b, s, c = q.shape
    m = s // dil
    tq = min(ATTN_ROWS, m)
    g = tq // BLOCK
    view = lambda t: t.reshape(b, m, dil * t.shape[-1])
    cur = lambda bi, r, i: (bi, i, r)
    prev = lambda bi, r, i: (bi, jnp.maximum(i * g - 1, 0), r)
    o, st = pl.pallas_call(
        _attn_kernel,
        grid=(b, dil, m // tq),
        in_specs=[
            pl.BlockSpec(bias.shape, lambda bi, r, i: (0, 0, 0)),
            pl.BlockSpec((1, tq, c), cur),
            pl.BlockSpec((1, tq, c), cur),
            pl.BlockSpec((1, BLOCK, c), prev),
            pl.BlockSpec((1, tq, c), cur),
            pl.BlockSpec((1, BLOCK, c), prev),
        ],
        out_specs=[pl.BlockSpec((1, tq, c), cur), pl.BlockSpec((1, tq, LANES), cur)],
        out_shape=[jax.ShapeDtypeStruct((b, m, dil * c), _BF16),
                   jax.ShapeDtypeStruct((b, m, dil * LANES), _F32)],
        scratch_shapes=[pltpu.VMEM((BLOCK + tq, c), _BF16), pltpu.VMEM((BLOCK + tq, c), _BF16)],
        compiler_params=pltpu.CompilerParams(
            dimension_semantics=("arbitrary", "arbitrary", "arbitrary"),
            vmem_limit_bytes=VMEM_LIMIT),
        name=f"attn_d{dil}",
    )(bias, view(q), view(k), view(k), view(v), view(v))
    return o.reshape(b, s, c), st.reshape(b, s, LANES)


def _out_proj_kernel(x_ref, mixc_ref, o1_ref, o2_ref, o3_ref, s1_ref, s2_ref, s3_ref,
                     expand_ref, gao_ref, w_ref, out_ref):
    lses = [s1_ref[0], s2_ref[0], s3_ref[0]]
    top = jnp.maximum(jnp.maximum(lses[0], lses[1]), lses[2])
    es = [jnp.exp(l - top) for l in lses]
    inv = 1.0 / (es[0] + es[1] + es[2])
    y = None
    for e, o_ref in zip(es, (o1_ref, o2_ref, o3_ref)):
        w = jnp.dot((e * inv).astype(_BF16), expand_ref[...], preferred_element_type=_F32)
        t = w * o_ref[0].astype(_F32)
        y = t if y is None else y + t
    ya = _rms(y, gao_ref[...]).astype(_BF16)
    mix = jnp.concatenate([mixc_ref[0], ya], axis=-1)
    out_ref[0] = x_ref[0] + jnp.dot(mix, w_ref[...], preferred_element_type=_F32)


def _out_proj(x, mixc, outs, stats, expand, g_attn_out, w_out):
    b, s, _ = x.shape
    tm = ROW_TILE
    row = lambda bi, si: (bi, si, 0)
    const = lambda bi, si: (0, 0)
    return pl.pallas_call(
        _out_proj_kernel,
        grid=(b, s // tm),
        in_specs=[pl.BlockSpec((1, tm, D_MODEL), row), pl.BlockSpec((1, tm, D_CONV), row)]
        + [pl.BlockSpec((1, tm, D_ATTN), row)] * 3
        + [pl.BlockSpec((1, tm, LANES), row)] * 3
        + [pl.BlockSpec(expand.shape, const), pl.BlockSpec((1, D_ATTN), const),
           pl.BlockSpec(w_out.shape, const)],
        out_specs=pl.BlockSpec((1, tm, D_MODEL), row),
        out_shape=jax.ShapeDtypeStruct(x.shape, _F32),
        compiler_params=pltpu.CompilerParams(
            dimension_semantics=("arbitrary", "arbitrary"), vmem_limit_bytes=VMEM_LIMIT),
        name="out_proj",
    )(x, mixc, *outs, *stats, expand, g_attn_out, w_out)


def _ffn_kernel(x_ref, g_ref, wg_ref, wu_ref, wd_ref, out_ref):
    x = x_ref[0]
    h = _rms(x, g_ref[...]).astype(_BF16)
    acc = x
    for c in range(D_FF // FF_CHUNK):
        cols = slice(c * FF_CHUNK, (c + 1) * FF_CHUNK)
        gate = jnp.dot(h, wg_ref[:, cols], preferred_element_type=_F32)
        up = jnp.dot(h, wu_ref[:, cols], preferred_element_type=_F32)
        act = (gate * jax.nn.sigmoid(gate) * up).astype(_BF16)
        acc = acc + jnp.dot(act, wd_ref[cols, :], preferred_element_type=_F32)
    out_ref[0] = acc


def _ffn(x, g_ffn, w_gate, w_up, w_down):
    b, s, _ = x.shape
    tm = ROW_TILE
    row = lambda bi, si: (bi, si, 0)
    const = lambda bi, si: (0, 0)
    resident = lambda a: pl.BlockSpec(a.shape, const, pipeline_mode=pl.Buffered(1))
    return pl.pallas_call(
        _ffn_kernel,
        grid=(b, s // tm),
        in_specs=[pl.BlockSpec((1, tm, D_MODEL), row), pl.BlockSpec((1, D_MODEL), const),
                  resident(w_gate), resident(w_up), resident(w_down)],
        out_specs=pl.BlockSpec((1, tm, D_MODEL), row),
        out_shape=jax.ShapeDtypeStruct(x.shape, _F32),
        compiler_params=pltpu.CompilerParams(
            dimension_semantics=("arbitrary", "arbitrary"), vmem_limit_bytes=VMEM_LIMIT),
        name="ffn",
    )(x, g_ffn, w_gate, w_up, w_down)


def kernel(x, g_mix, w_in, conv_w, g_q, g_k, g_conv_out, g_attn_out, w_out, g_ffn, w_gate, w_up, w_down):
    depth = w_in.shape[0]
    bias = _band_bias()
    expand = (jnp.arange(LANES)[:, None] == jnp.arange(D_ATTN)[None, :] // HEAD_DIM).astype(_BF16)
    for l in range(depth):
        gq_t = jnp.tile(g_q[l] * (HEAD_DIM ** -0.5), N_HEADS)[None, :]
        gk_t = jnp.tile(g_k[l], N_HEADS)[None, :]
        mixc, q, k, v = _in_proj(x, g_mix[l][None, :], w_in[l].astype(_BF16), conv_w[l],
                                 gq_t, gk_t, g_conv_out[l][None, :])
        outs, stats = [], []
        for _, dil in DILATED_BRANCHES:
            o, st = _attn_branch(bias, q, k, v, dil)
            outs.append(o)
            stats.append(st)
        x = _out_proj(x, mixc, outs, stats, expand, g_attn_out[l][None, :], w_out[l].astype(_BF16))
        x = _ffn(x, g_ffn[l][None, :], w_gate[l].astype(_BF16), w_up[l].astype(_BF16),
                 w_down[l].astype(_BF16))
    return x
```

```python
import jax
import jax.numpy as jnp
from jax import lax
from jax.experimental import pallas as pl
from jax.experimental.pallas import tpu as pltpu

D_MODEL = 1024
HEAD_DIM = 64
D_CONV = 512
D_ATTN = 512
N_HEADS = D_ATTN // HEAD_DIM
DILATIONS = (1, 4, 16)
BLOCK = 128
D_FF = 2816
EPS = 1e-6

LANES = 128
SUBLANES = 8
PAIR = 2 * HEAD_DIM
N_PAIRS = D_ATTN // PAIR
ROW_TILE = 512
ATTN_ROWS = 512
FF_CHUNK = 1408
VMEM_LIMIT = 56 * 1024 * 1024

assert PAIR == LANES and D_FF % FF_CHUNK == 0 and FF_CHUNK % LANES == 0

_F32 = jnp.float32
_BF16 = jnp.bfloat16


def _rms(x, g):
    ms = jnp.mean(x * x, axis=-1, keepdims=True)
    return x * lax.rsqrt(ms + EPS) * g


def _head_rms(z, g_tiled):
    lo = lax.broadcasted_iota(jnp.int32, (1, PAIR), 1) < HEAD_DIM
    outs = []
    for p in range(N_PAIRS):
        z2 = z[:, p * PAIR:(p + 1) * PAIR]
        sq = z2 * z2
        s_lo = jnp.sum(jnp.where(lo, sq, 0.0), axis=-1, keepdims=True)
        s_hi = jnp.sum(jnp.where(lo, 0.0, sq), axis=-1, keepdims=True)
        ms = jnp.where(lo, s_lo, s_hi) * (1.0 / HEAD_DIM)
        outs.append(z2 * lax.rsqrt(ms + EPS))
    return jnp.concatenate(outs, axis=-1) * g_tiled


def _in_proj_kernel(x_ref, gmix_ref, w_ref, cw_ref, gq_ref, gk_ref, gco_ref,
                    mixc_ref, q1, k1, v1, q4, k4, v4, q16, k16, v16, cbuf, pbuf):
    s_idx = pl.program_id(1)
    tm = x_ref.shape[1]
    h = _rms(x_ref[0], gmix_ref[...]).astype(_BF16)

    def proj(c):
        return jnp.dot(h, w_ref[:, c * D_CONV:(c + 1) * D_CONV], preferred_element_type=_F32)

    @pl.when(s_idx == 0)
    def _():
        cbuf[0:SUBLANES, :] = jnp.zeros((SUBLANES, D_CONV), _F32)

    @pl.when(s_idx > 0)
    def _():
        cbuf[0:SUBLANES, :] = cbuf[tm:tm + SUBLANES, :]

    cu = proj(2) * proj(0)
    cbuf[SUBLANES:SUBLANES + tm, :] = cu
    c1 = cbuf[SUBLANES - 1:SUBLANES - 1 + tm, :]
    c2 = cbuf[SUBLANES - 2:SUBLANES - 2 + tm, :]
    cw = cw_ref[...]
    y = proj(1) * (cw[0:1, :] * c2 + cw[1:2, :] * c1 + cw[2:3, :] * cu)
    mixc_ref[0] = _rms(y, gco_ref[...]).astype(_BF16)

    def emit(slot, z, nat_ref, perm_refs):
        nat_ref[0] = z.astype(_BF16)
        for g in range(N_PAIRS):
            pbuf[slot * N_PAIRS + g] = z[:, g * LANES:(g + 1) * LANES]
        for d, ref in perm_refs:
            n = tm // d
            for r in range(d):
                for g in range(N_PAIRS):
                    ref[0, r, :, g * LANES:(g + 1) * LANES] = (
                        pbuf[slot * N_PAIRS + g, pl.ds(r, n, stride=d), :].astype(_BF16))

    emit(0, _head_rms(proj(3), gq_ref[...]), q1, ((4, q4), (16, q16)))
    emit(1, _head_rms(proj(4), gk_ref[...]), k1, ((4, k4), (16, k16)))
    emit(2, proj(5), v1, ((4, v4), (16, v16)))


def _in_proj(x, g_mix, w_in, conv_w, gq_t, gk_t, g_conv_out):
    b, s, _ = x.shape
    tm = ROW_TILE
    row = lambda bi, si: (bi, si, 0)
    const = lambda bi, si: (0, 0)
    nat_sds = jax.ShapeDtypeStruct((b, s, D_CONV), _BF16)
    out_specs = [pl.BlockSpec((1, tm, D_CONV), row)] * 4
    out_shape = [nat_sds] * 4
    for d in DILATIONS[1:]:
        out_specs += [pl.BlockSpec((1, d, tm // d, D_ATTN), lambda bi, si: (bi, 0, si, 0))] * 3
        out_shape += [jax.ShapeDtypeStruct((b, d, s // d, D_ATTN), _BF16)] * 3
    return pl.pallas_call(
        _in_proj_kernel,
        grid=(b, s // tm),
        in_specs=[
            pl.BlockSpec((1, tm, D_MODEL), row),
            pl.BlockSpec((1, D_MODEL), const),
            pl.BlockSpec(w_in.shape, const),
            pl.BlockSpec(conv_w.shape, const),
            pl.BlockSpec((1, D_ATTN), const),
            pl.BlockSpec((1, D_ATTN), const),
            pl.BlockSpec((1, D_CONV), const),
        ],
        out_specs=out_specs,
        out_shape=out_shape,
        scratch_shapes=[pltpu.VMEM((SUBLANES + tm, D_CONV), _F32),
                        pltpu.VMEM((3 * N_PAIRS, tm, LANES), _F32)],
        compiler_params=pltpu.CompilerParams(
            dimension_semantics=("arbitrary", "arbitrary"), vmem_limit_bytes=VMEM_LIMIT),
        name="in_proj",
    )(x, g_mix, w_in, conv_w, gq_t, gk_t, g_conv_out)


def _attn_kernel(bias_ref, q_ref, kc_ref, kp_ref, vc_ref, vp_ref, o_ref, st_ref, kbuf, vbuf):
    i = pl.program_id(2)
    tq = q_ref.shape[0]
    kbuf[0:BLOCK, :] = kp_ref[...]
    kbuf[BLOCK:BLOCK + tq, :] = kc_ref[...]
    vbuf[0:BLOCK, :] = vp_ref[...]
    vbuf[BLOCK:BLOCK + tq, :] = vc_ref[...]
    lane = lax.broadcasted_iota(jnp.int32, (1, LANES), 1)
    lo = lane < HEAD_DIM

    def q_block(j, carry):
        r0 = pl.multiple_of(j * BLOCK, BLOCK)
        first = jnp.logical_and(i == 0, j == 0).astype(jnp.int32)
        bias = bias_ref[first]
        stats = jnp.zeros((BLOCK, LANES), _F32)
        for p in range(N_PAIRS):
            cols = slice(p * PAIR, (p + 1) * PAIR)
            q2 = q_ref[pl.ds(r0, BLOCK), cols]
            k2 = kbuf[pl.ds(r0, 2 * BLOCK), cols]
            v2 = vbuf[pl.ds(r0, 2 * BLOCK), cols]
            outs = []
            for hh in range(2):
                qm = jnp.where(lo if hh == 0 else jnp.logical_not(lo), q2, jnp.zeros_like(q2))
                s = lax.dot_general(qm, k2, (((1,), (1,)), ((), ())),
                                    preferred_element_type=_F32) + bias
                m = jnp.max(s, axis=-1, keepdims=True)
                e = jnp.exp(s - m)
                den = jnp.sum(e, axis=-1, keepdims=True)
                pv = jnp.dot(e.astype(_BF16), v2, preferred_element_type=_F32)
                outs.append(pv * (1.0 / den))
                stats = jnp.where(lane == 2 * p + hh, m + jnp.log(den), stats)
            o_ref[pl.ds(r0, BLOCK), cols] = jnp.where(lo, outs[0], outs[1]).astype(_BF16)
        st_ref[pl.ds(r0, BLOCK), :] = stats
        return carry

    lax.fori_loop(0, tq // BLOCK, q_block, 0)


def _band_bias():
    i = jnp.arange(BLOCK)[:, None]
    j = jnp.arange(2 * BLOCK)[None, :]
    dist = BLOCK + i - j
    ok = (dist >= 0) & (dist <= BLOCK)
    ok_first = ok & (j >= BLOCK)
    neg = jnp.float32(-jnp.inf)
    return jnp.stack([jnp.where(ok, 0.0, neg), jnp.where(ok_first, 0.0, neg)]).astype(_F32)


def _attn_branch(bias, q, k, v):
    b, dil, m, c = q.shape
    tq = min(ATTN_ROWS, m)
    g = tq // BLOCK
    cur = lambda bi, r, i: (bi, r, i, 0)
    prev = lambda bi, r, i: (bi, r, jnp.maximum(i * g - 1, 0), 0)
    return pl.pallas_call(
        _attn_kernel,
        grid=(b, dil, m // tq),
        in_specs=[
            pl.BlockSpec(bias.shape, lambda bi, r, i: (0, 0, 0)),
            pl.BlockSpec((None, None, tq, c), cur),
            pl.BlockSpec((None, None, tq, c), cur),
            pl.BlockSpec((None, None, BLOCK, c), prev),
            pl.BlockSpec((None, None, tq, c), cur),
            pl.BlockSpec((None, None, BLOCK, c), prev),
        ],
        out_specs=[pl.BlockSpec((None, None, tq, c), cur),
                   pl.BlockSpec((None, None, tq, LANES), cur)],
        out_shape=[jax.ShapeDtypeStruct((b, dil, m, c), _BF16),
                   jax.ShapeDtypeStruct((b, dil, m, LANES), _F32)],
        scratch_shapes=[pltpu.VMEM((BLOCK + tq, c), _BF16), pltpu.VMEM((BLOCK + tq, c), _BF16)],
        compiler_params=pltpu.CompilerParams(
            dimension_semantics=("arbitrary", "arbitrary", "arbitrary"),
            vmem_limit_bytes=VMEM_LIMIT),
        name=f"attn_d{dil}",
    )(bias, q, k, k, v, v)


def _out_ffn_kernel(x_ref, mixc_ref, o1_ref, o4_ref, o16_ref, s1_ref, s4_ref, s16_ref,
                    expand_ref, gao_ref, wo_ref, gffn_ref, wg_ref, wu_ref, wd_ref,
                    out_ref, obuf, sbuf):
    tm = x_ref.shape[1]

    def natural(slot, o_ref, s_ref, d):
        if d == 1:
            return o_ref[0, 0].astype(_F32), s_ref[0, 0]
        n = tm // d
        for r in range(d):
            for g in range(N_PAIRS):
                obuf[slot * N_PAIRS + g, pl.ds(r, n, stride=d), :] = (
                    o_ref[0, r, :, g * LANES:(g + 1) * LANES].astype(_F32))
            sbuf[slot, pl.ds(r, n, stride=d), :] = s_ref[0, r]
        o = jnp.concatenate([obuf[slot * N_PAIRS + g] for g in range(N_PAIRS)], axis=-1)
        return o, sbuf[slot]

    branches = [natural(0, o1_ref, s1_ref, 1), natural(0, o4_ref, s4_ref, 4),
                natural(1, o16_ref, s16_ref, 16)]
    lses = [st for _, st in branches]
    top = jnp.maximum(jnp.maximum(lses[0], lses[1]), lses[2])
    es = [jnp.exp(l - top) for l in lses]
    inv = 1.0 / (es[0] + es[1] + es[2])
    y = None
    for e, (o, _) in zip(es, branches):
        w = jnp.dot((e * inv).astype(_BF16), expand_ref[...], preferred_element_type=_F32)
        y = w * o if y is None else y + w * o
    ya = _rms(y, gao_ref[...]).astype(_BF16)
    mix = jnp.concatenate([mixc_ref[0], ya], axis=-1)
    x1 = x_ref[0] + jnp.dot(mix, wo_ref[...], preferred_element_type=_F32)

    h = _rms(x1, gffn_ref[...]).astype(_BF16)
    acc = x1
    for c in range(D_FF // FF_CHUNK):
        cols = slice(c * FF_CHUNK, (c + 1) * FF_CHUNK)
        gate = jnp.dot(h, wg_ref[:, cols], preferred_element_type=_F32)
        up = jnp.dot(h, wu_ref[:, cols], preferred_element_type=_F32)
        act = (gate * jax.nn.sigmoid(gate) * up).astype(_BF16)
        acc = acc + jnp.dot(act, wd_ref[cols, :], preferred_element_type=_F32)
    out_ref[0] = acc


def _out_ffn(x, mixc, outs, stats, expand, g_attn_out, w_out, g_ffn, w_gate, w_up, w_down):
    b, s, _ = x.shape
    tm = ROW_TILE
    row = lambda bi, si: (bi, si, 0)
    const = lambda bi, si: (0, 0)
    perm = lambda bi, si: (bi, 0, si, 0)
    resident = lambda a: pl.BlockSpec(a.shape, const, pipeline_mode=pl.Buffered(1))
    in_specs = [pl.BlockSpec((1, tm, D_MODEL), row), pl.BlockSpec((1, tm, D_CONV), row)]
    in_specs += [pl.BlockSpec((1, d, tm // d, D_ATTN), perm) for d in DILATIONS]
    in_specs += [pl.BlockSpec((1, d, tm // d, LANES), perm) for d in DILATIONS]
    in_specs += [pl.BlockSpec(expand.shape, const), pl.BlockSpec((1, D_ATTN), const),
                 resident(w_out), pl.BlockSpec((1, D_MODEL), const),
                 resident(w_gate), resident(w_up), resident(w_down)]
    return pl.pallas_call(
        _out_ffn_kernel,
        grid=(b, s // tm),
        in_specs=in_specs,
        out_specs=pl.BlockSpec((1, tm, D_MODEL), row),
        out_shape=jax.ShapeDtypeStruct(x.shape, _F32),
        scratch_shapes=[pltpu.VMEM((2 * N_PAIRS, tm, LANES), _F32),
                        pltpu.VMEM((2, tm, LANES), _F32)],
        compiler_params=pltpu.CompilerParams(
            dimension_semantics=("arbitrary", "arbitrary"), vmem_limit_bytes=VMEM_LIMIT),
        name="out_ffn",
    )(x, mixc, *outs, *stats, expand, g_attn_out, w_out, g_ffn, w_gate, w_up, w_down)


def kernel(x, g_mix, w_in, conv_w, g_q, g_k, g_conv_out, g_attn_out, w_out, g_ffn, w_gate, w_up, w_down):
    depth = w_in.shape[0]
    bias = _band_bias()
    expand = (jnp.arange(LANES)[:, None] == jnp.arange(D_ATTN)[None, :] // HEAD_DIM).astype(_BF16)
    for l in range(depth):
        gq_t = jnp.tile(g_q[l] * (HEAD_DIM ** -0.5), N_HEADS)[None, :]
        gk_t = jnp.tile(g_k[l], N_HEADS)[None, :]
        mixc, q1, k1, v1, q4, k4, v4, q16, k16, v16 = _in_proj(
            x, g_mix[l][None, :], w_in[l].astype(_BF16), conv_w[l], gq_t, gk_t,
            g_conv_out[l][None, :])
        outs, stats = [], []
        for q, k, v in ((q1[:, None], k1[:, None], v1[:, None]), (q4, k4, v4), (q16, k16, v16)):
            o, st = _attn_branch(bias, q, k, v)
            outs.append(o)
            stats.append(st)
        x = _out_ffn(x, mixc, outs, stats, expand, g_attn_out[l][None, :], w_out[l].astype(_BF16),
                     g_ffn[l][None, :], w_gate[l].astype(_BF16), w_up[l].astype(_BF16),
                     w_down[l].astype(_BF16))
    return x
```

```python
import jax
import jax.numpy as jnp
from jax import lax
from jax.experimental import pallas as pl
from jax.experimental.pallas import tpu as pltpu

D_MODEL = 1024
HEAD_DIM = 64
D_CONV = 512
D_ATTN = 512
N_HEADS = D_ATTN // HEAD_DIM
DILATIONS = (1, 4, 16)
BLOCK = 128
D_FF = 2816
EPS = 1e-6
LOG2E = 1.4426950408889634
LN2 = 0.6931471805599453

LANES = 128
SUBLANES = 8
PAIR = 2 * HEAD_DIM
N_PAIRS = D_ATTN // PAIR
ROW_TILE = 512
ATTN_ROWS = 512
FF_CHUNK = 1408
VMEM_LIMIT = 56 * 1024 * 1024

assert PAIR == LANES and D_FF % FF_CHUNK == 0 and FF_CHUNK % LANES == 0

_F32 = jnp.float32
_BF16 = jnp.bfloat16


def _rms(x, g):
    ms = jnp.mean(x * x, axis=-1, keepdims=True)
    return x * lax.rsqrt(ms + EPS) * g


def _head_rms(z, g_tiled):
    lo = lax.broadcasted_iota(jnp.int32, (1, PAIR), 1) < HEAD_DIM
    outs = []
    for p in range(N_PAIRS):
        z2 = z[:, p * PAIR:(p + 1) * PAIR]
        sq = z2 * z2
        s_lo = jnp.sum(jnp.where(lo, sq, 0.0), axis=-1, keepdims=True)
        s_hi = jnp.sum(jnp.where(lo, 0.0, sq), axis=-1, keepdims=True)
        ms = jnp.where(lo, s_lo, s_hi) * (1.0 / HEAD_DIM)
        outs.append(z2 * lax.rsqrt(ms + EPS))
    return jnp.concatenate(outs, axis=-1) * g_tiled


def _in_proj_kernel(x_ref, gmix_ref, w_ref, cw_ref, gq_ref, gk_ref, gco_ref,
                    mixc_ref, q1, k1, v1, q4, k4, v4, q16, k16, v16, cbuf, pbuf):
    s_idx = pl.program_id(1)
    tm = x_ref.shape[1]
    h = _rms(x_ref[0], gmix_ref[...]).astype(_BF16)

    def proj(c):
        return jnp.dot(h, w_ref[:, c * D_CONV:(c + 1) * D_CONV], preferred_element_type=_F32)

    @pl.when(s_idx == 0)
    def _():
        cbuf[0:SUBLANES, :] = jnp.zeros((SUBLANES, D_CONV), _F32)

    @pl.when(s_idx > 0)
    def _():
        cbuf[0:SUBLANES, :] = cbuf[tm:tm + SUBLANES, :]

    cu = proj(2) * proj(0)
    cbuf[SUBLANES:SUBLANES + tm, :] = cu
    c1 = cbuf[SUBLANES - 1:SUBLANES - 1 + tm, :]
    c2 = cbuf[SUBLANES - 2:SUBLANES - 2 + tm, :]
    cw = cw_ref[...]
    y = proj(1) * (cw[0:1, :] * c2 + cw[1:2, :] * c1 + cw[2:3, :] * cu)
    mixc_ref[0] = _rms(y, gco_ref[...]).astype(_BF16)

    def emit(slot, z, nat_ref, perm_refs):
        nat_ref[0] = z.astype(_BF16)
        for g in range(N_PAIRS):
            pbuf[slot * N_PAIRS + g] = z[:, g * LANES:(g + 1) * LANES]
        for d, ref in perm_refs:
            n = tm // d
            for r in range(d):
                for g in range(N_PAIRS):
                    ref[0, r, :, g * LANES:(g + 1) * LANES] = (
                        pbuf[slot * N_PAIRS + g, pl.ds(r, n, stride=d), :].astype(_BF16))

    emit(0, _head_rms(proj(3), gq_ref[...]), q1, ((4, q4), (16, q16)))
    emit(1, _head_rms(proj(4), gk_ref[...]), k1, ((4, k4), (16, k16)))
    emit(2, proj(5), v1, ((4, v4), (16, v16)))


def _in_proj(x, g_mix, w_in, conv_w, gq_t, gk_t, g_conv_out):
    b, s, _ = x.shape
    tm = ROW_TILE
    row = lambda bi, si: (bi, si, 0)
    const = lambda bi, si: (0, 0)
    nat_sds = jax.ShapeDtypeStruct((b, s, D_CONV), _BF16)
    out_specs = [pl.BlockSpec((1, tm, D_CONV), row)] * 4
    out_shape = [nat_sds] * 4
    for d in DILATIONS[1:]:
        out_specs += [pl.BlockSpec((1, d, tm // d, D_ATTN), lambda bi, si: (bi, 0, si, 0))] * 3
        out_shape += [jax.ShapeDtypeStruct((b, d, s // d, D_ATTN), _BF16)] * 3
    return pl.pallas_call(
        _in_proj_kernel,
        grid=(b, s // tm),
        in_specs=[
            pl.BlockSpec((1, tm, D_MODEL), row),
            pl.BlockSpec((1, D_MODEL), const),
            pl.BlockSpec(w_in.shape, const),
            pl.BlockSpec(conv_w.shape, const),
            pl.BlockSpec((1, D_ATTN), const),
            pl.BlockSpec((1, D_ATTN), const),
            pl.BlockSpec((1, D_CONV), const),
        ],
        out_specs=out_specs,
        out_shape=out_shape,
        scratch_shapes=[pltpu.VMEM((SUBLANES + tm, D_CONV), _F32),
                        pltpu.VMEM((3 * N_PAIRS, tm, LANES), _F32)],
        compiler_params=pltpu.CompilerParams(
            dimension_semantics=("arbitrary", "arbitrary"), vmem_limit_bytes=VMEM_LIMIT),
        name="in_proj",
    )(x, g_mix, w_in, conv_w, gq_t, gk_t, g_conv_out)


def _attn_kernel(bias_ref, q_ref, kc_ref, kp_ref, vc_ref, vp_ref, o_ref, st_ref, kbuf, vbuf):
    i = pl.program_id(2)
    tq = q_ref.shape[0]
    kbuf[0:BLOCK, :] = kp_ref[...]
    kbuf[BLOCK:BLOCK + tq, :] = kc_ref[...]
    vbuf[0:BLOCK, :] = vp_ref[...]
    vbuf[BLOCK:BLOCK + tq, :] = vc_ref[...]
    lo = lax.broadcasted_iota(jnp.int32, (1, LANES), 1) < HEAD_DIM

    def q_block(j, carry):
        r0 = pl.multiple_of(j * BLOCK, BLOCK)
        first = jnp.logical_and(i == 0, j == 0).astype(jnp.int32)
        bias = bias_ref[first]
        for p in range(N_PAIRS):
            cols = slice(p * PAIR, (p + 1) * PAIR)
            q2 = q_ref[pl.ds(r0, BLOCK), cols]
            k2 = kbuf[pl.ds(r0, 2 * BLOCK), cols]
            v2 = vbuf[pl.ds(r0, 2 * BLOCK), cols]
            ms, dens, pvs = [], [], []
            for hh in range(2):
                qm = jnp.where(lo if hh == 0 else jnp.logical_not(lo), q2, jnp.zeros_like(q2))
                s = lax.dot_general(qm, k2, (((1,), (1,)), ((), ())),
                                    preferred_element_type=_F32) + bias
                m = jnp.max(s, axis=-1, keepdims=True)
                e = jnp.exp2(s - m)
                ms.append(m)
                dens.append(jnp.sum(e, axis=-1, keepdims=True))
                pvs.append(jnp.dot(e.astype(_BF16), v2, preferred_element_type=_F32))
            den = jnp.where(lo, dens[0], dens[1])
            o_ref[pl.ds(r0, BLOCK), cols] = (jnp.where(lo, pvs[0], pvs[1]) * (1.0 / den)).astype(_BF16)
            st_ref[pl.ds(r0, BLOCK), cols] = jnp.where(lo, ms[0], ms[1]) * LN2 + jnp.log(den)
        return carry

    lax.fori_loop(0, tq // BLOCK, q_block, 0, unroll=True)


def _band_bias():
    i = jnp.arange(BLOCK)[:, None]
    j = jnp.arange(2 * BLOCK)[None, :]
    dist = BLOCK + i - j
    ok = (dist >= 0) & (dist <= BLOCK)
    ok_first = ok & (j >= BLOCK)
    neg = jnp.float32(-jnp.inf)
    return jnp.stack([jnp.where(ok, 0.0, neg), jnp.where(ok_first, 0.0, neg)]).astype(_F32)


def _attn_branch(bias, q, k, v):
    b, dil, m, c = q.shape
    tq = min(ATTN_ROWS, m)
    g = tq // BLOCK
    cur = lambda bi, r, i: (bi, r, i, 0)
    prev = lambda bi, r, i: (bi, r, jnp.maximum(i * g - 1, 0), 0)
    return pl.pallas_call(
        _attn_kernel,
        grid=(b, dil, m // tq),
        in_specs=[
            pl.BlockSpec(bias.shape, lambda bi, r, i: (0, 0, 0)),
            pl.BlockSpec((None, None, tq, c), cur),
            pl.BlockSpec((None, None, tq, c), cur),
            pl.BlockSpec((None, None, BLOCK, c), prev),
            pl.BlockSpec((None, None, tq, c), cur),
            pl.BlockSpec((None, None, BLOCK, c), prev),
        ],
        out_specs=[pl.BlockSpec((None, None, tq, c), cur),
                   pl.BlockSpec((None, None, tq, c), cur)],
        out_shape=[jax.ShapeDtypeStruct((b, dil, m, c), _BF16),
                   jax.ShapeDtypeStruct((b, dil, m, c), _F32)],
        scratch_shapes=[pltpu.VMEM((BLOCK + tq, c), _BF16), pltpu.VMEM((BLOCK + tq, c), _BF16)],
        compiler_params=pltpu.CompilerParams(
            dimension_semantics=("arbitrary", "arbitrary", "arbitrary"),
            vmem_limit_bytes=VMEM_LIMIT),
        name=f"attn_d{dil}",
    )(bias, q, k, k, v, v)


def _out_ffn_kernel(x_ref, mixc_ref, o1_ref, o4_ref, o16_ref, s1_ref, s4_ref, s16_ref,
                    gao_ref, wo_ref, gffn_ref, wg_ref, wu_ref, wd_ref, out_ref, obuf, sbuf):
    tm = x_ref.shape[1]

    def natural(slot, o_ref, s_ref, d):
        if d == 1:
            return o_ref[0, 0].astype(_F32), s_ref[0, 0]
        n = tm // d
        for r in range(d):
            for g in range(N_PAIRS):
                plane, lanes = slot * N_PAIRS + g, slice(g * LANES, (g + 1) * LANES)
                obuf[plane, pl.ds(r, n, stride=d), :] = o_ref[0, r, :, lanes].astype(_F32)
                sbuf[plane, pl.ds(r, n, stride=d), :] = s_ref[0, r, :, lanes]
        planes = range(slot * N_PAIRS, (slot + 1) * N_PAIRS)
        return (jnp.concatenate([obuf[g] for g in planes], axis=-1),
                jnp.concatenate([sbuf[g] for g in planes], axis=-1))

    branches = [natural(0, o1_ref, s1_ref, 1), natural(0, o4_ref, s4_ref, 4),
                natural(1, o16_ref, s16_ref, 16)]
    lses = [st for _, st in branches]
    top = jnp.maximum(jnp.maximum(lses[0], lses[1]), lses[2])
    es = [jnp.exp(l - top) for l in lses]
    y = (es[0] * branches[0][0] + es[1] * branches[1][0] + es[2] * branches[2][0]) * (
        1.0 / (es[0] + es[1] + es[2]))
    ya = _rms(y, gao_ref[...]).astype(_BF16)
    mix = jnp.concatenate([mixc_ref[0], ya], axis=-1)
    x1 = x_ref[0] + jnp.dot(mix, wo_ref[...], preferred_element_type=_F32)

    h = _rms(x1, gffn_ref[...]).astype(_BF16)
    acc = x1
    for c in range(D_FF // FF_CHUNK):
        cols = slice(c * FF_CHUNK, (c + 1) * FF_CHUNK)
        gate = jnp.dot(h, wg_ref[:, cols], preferred_element_type=_F32)
        up = jnp.dot(h, wu_ref[:, cols], preferred_element_type=_F32)
        act = (gate * jax.nn.sigmoid(gate) * up).astype(_BF16)
        acc = acc + jnp.dot(act, wd_ref[cols, :], preferred_element_type=_F32)
    out_ref[0] = acc


def _out_ffn(x, mixc, outs, stats, g_attn_out, w_out, g_ffn, w_gate, w_up, w_down):
    b, s, _ = x.shape
    tm = ROW_TILE
    row = lambda bi, si: (bi, si, 0)
    const = lambda bi, si: (0, 0)
    perm = lambda bi, si: (bi, 0, si, 0)
    resident = lambda a: pl.BlockSpec(a.shape, const, pipeline_mode=pl.Buffered(1))
    in_specs = [pl.BlockSpec((1, tm, D_MODEL), row), pl.BlockSpec((1, tm, D_CONV), row)]
    in_specs += [pl.BlockSpec((1, d, tm // d, D_ATTN), perm) for d in DILATIONS]
    in_specs += [pl.BlockSpec((1, d, tm // d, D_ATTN), perm) for d in DILATIONS]
    in_specs += [pl.BlockSpec((1, D_ATTN), const),
                 resident(w_out), pl.BlockSpec((1, D_MODEL), const),
                 resident(w_gate), resident(w_up), resident(w_down)]
    return pl.pallas_call(
        _out_ffn_kernel,
        grid=(b, s // tm),
        in_specs=in_specs,
        out_specs=pl.BlockSpec((1, tm, D_MODEL), row),
        out_shape=jax.ShapeDtypeStruct(x.shape, _F32),
        scratch_shapes=[pltpu.VMEM((2 * N_PAIRS, tm, LANES), _F32),
                        pltpu.VMEM((2 * N_PAIRS, tm, LANES), _F32)],
        compiler_params=pltpu.CompilerParams(
            dimension_semantics=("arbitrary", "arbitrary"), vmem_limit_bytes=VMEM_LIMIT),
        name="out_ffn",
    )(x, mixc, *outs, *stats, g_attn_out, w_out, g_ffn, w_gate, w_up, w_down)


def kernel(x, g_mix, w_in, conv_w, g_q, g_k, g_conv_out, g_attn_out, w_out, g_ffn, w_gate, w_up, w_down):
    depth = w_in.shape[0]
    bias = _band_bias()
    for l in range(depth):
        gq_t = jnp.tile(g_q[l] * (HEAD_DIM ** -0.5 * LOG2E), N_HEADS)[None, :]
        gk_t = jnp.tile(g_k[l], N_HEADS)[None, :]
        mixc, q1, k1, v1, q4, k4, v4, q16, k16, v16 = _in_proj(
            x, g_mix[l][None, :], w_in[l].astype(_BF16), conv_w[l], gq_t, gk_t,
            g_conv_out[l][None, :])
        outs, stats = [], []
        for q, k, v in ((q1[:, None], k1[:, None], v1[:, None]), (q4, k4, v4), (q16, k16, v16)):
            o, st = _attn_branch(bias, q, k, v)
            outs.append(o)
            stats.append(st)
        x = _out_ffn(x, mixc, outs, stats, g_attn_out[l][None, :], w_out[l].astype(_BF16),
                     g_ffn[l][None, :], w_gate[l].astype(_BF16), w_up[l].astype(_BF16),
                     w_down[l].astype(_BF16))
    return x
```

```python
import jax
import jax.numpy as jnp
from jax import lax
from jax.experimental import pallas as pl
from jax.experimental.pallas import tpu as pltpu

D_MODEL = 1024
HEAD_DIM = 64
D_CONV = 512
D_ATTN = 512
N_HEADS = D_ATTN // HEAD_DIM
DILATIONS = (1, 4, 16)
BLOCK = 128
D_FF = 2816
EPS = 1e-6
LOG2E = 1.4426950408889634
LN2 = 0.6931471805599453

LANES = 128
SUBLANES = 8
PAIR = 2 * HEAD_DIM
N_PAIRS = D_ATTN // PAIR
ROW_TILE = 512
ATTN_ROWS = 2048
FF_CHUNK = 1408
VMEM_LIMIT = 56 * 1024 * 1024

assert PAIR == LANES and D_FF % FF_CHUNK == 0 and FF_CHUNK % LANES == 0

_F32 = jnp.float32
_BF16 = jnp.bfloat16


def _rms(x, g):
    ms = jnp.mean(x * x, axis=-1, keepdims=True)
    return x * lax.rsqrt(ms + EPS) * g


def _head_rms(z, g_tiled):
    lo = lax.broadcasted_iota(jnp.int32, (1, PAIR), 1) < HEAD_DIM
    outs = []
    for p in range(N_PAIRS):
        z2 = z[:, p * PAIR:(p + 1) * PAIR]
        sq = z2 * z2
        s_lo = jnp.sum(jnp.where(lo, sq, 0.0), axis=-1, keepdims=True)
        s_hi = jnp.sum(jnp.where(lo, 0.0, sq), axis=-1, keepdims=True)
        ms = jnp.where(lo, s_lo, s_hi) * (1.0 / HEAD_DIM)
        outs.append(z2 * lax.rsqrt(ms + EPS))
    return jnp.concatenate(outs, axis=-1) * g_tiled


def _in_proj_kernel(x_ref, gmix_ref, w_ref, cw_ref, gq_ref, gk_ref, gco_ref,
                    mixc_ref, q1, k1, v1, q4, k4, v4, q16, k16, v16, cbuf, pbuf):
    s_idx = pl.program_id(1)
    tm = x_ref.shape[1]
    h = _rms(x_ref[0], gmix_ref[...]).astype(_BF16)

    def proj(c):
        return jnp.dot(h, w_ref[:, c * D_CONV:(c + 1) * D_CONV], preferred_element_type=_F32)

    @pl.when(s_idx == 0)
    def _():
        cbuf[0:SUBLANES, :] = jnp.zeros((SUBLANES, D_CONV), _F32)

    @pl.when(s_idx > 0)
    def _():
        cbuf[0:SUBLANES, :] = cbuf[tm:tm + SUBLANES, :]

    cu = proj(2) * proj(0)
    cbuf[SUBLANES:SUBLANES + tm, :] = cu
    c1 = cbuf[SUBLANES - 1:SUBLANES - 1 + tm, :]
    c2 = cbuf[SUBLANES - 2:SUBLANES - 2 + tm, :]
    cw = cw_ref[...]
    y = proj(1) * (cw[0:1, :] * c2 + cw[1:2, :] * c1 + cw[2:3, :] * cu)
    mixc_ref[0] = _rms(y, gco_ref[...]).astype(_BF16)

    def emit(slot, z, nat_ref, perm_refs):
        nat_ref[0] = z.astype(_BF16)
        for g in range(N_PAIRS):
            pbuf[slot * N_PAIRS + g] = z[:, g * LANES:(g + 1) * LANES]
        for d, ref in perm_refs:
            n = tm // d
            for r in range(d):
                for g in range(N_PAIRS):
                    ref[0, r, :, g * LANES:(g + 1) * LANES] = (
                        pbuf[slot * N_PAIRS + g, pl.ds(r, n, stride=d), :].astype(_BF16))

    emit(0, _head_rms(proj(3), gq_ref[...]), q1, ((4, q4), (16, q16)))
    emit(1, _head_rms(proj(4), gk_ref[...]), k1, ((4, k4), (16, k16)))
    emit(2, proj(5), v1, ((4, v4), (16, v16)))


def _in_proj(x, g_mix, w_in, conv_w, gq_t, gk_t, g_conv_out):
    b, s, _ = x.shape
    tm = ROW_TILE
    row = lambda bi, si: (bi, si, 0)
    const = lambda bi, si: (0, 0)
    nat_sds = jax.ShapeDtypeStruct((b, s, D_CONV), _BF16)
    out_specs = [pl.BlockSpec((1, tm, D_CONV), row)] * 4
    out_shape = [nat_sds] * 4
    for d in DILATIONS[1:]:
        out_specs += [pl.BlockSpec((1, d, tm // d, D_ATTN), lambda bi, si: (bi, 0, si, 0))] * 3
        out_shape += [jax.ShapeDtypeStruct((b, d, s // d, D_ATTN), _BF16)] * 3
    return pl.pallas_call(
        _in_proj_kernel,
        grid=(b, s // tm),
        in_specs=[
            pl.BlockSpec((1, tm, D_MODEL), row),
            pl.BlockSpec((1, D_MODEL), const),
            pl.BlockSpec(w_in.shape, const),
            pl.BlockSpec(conv_w.shape, const),
            pl.BlockSpec((1, D_ATTN), const),
            pl.BlockSpec((1, D_ATTN), const),
            pl.BlockSpec((1, D_CONV), const),
        ],
        out_specs=out_specs,
        out_shape=out_shape,
        scratch_shapes=[pltpu.VMEM((SUBLANES + tm, D_CONV), _F32),
                        pltpu.VMEM((3 * N_PAIRS, tm, LANES), _F32)],
        compiler_params=pltpu.CompilerParams(
            dimension_semantics=("arbitrary", "arbitrary"), vmem_limit_bytes=VMEM_LIMIT),
        name="in_proj",
    )(x, g_mix, w_in, conv_w, gq_t, gk_t, g_conv_out)


def _attn_kernel(bias_ref, q_ref, kc_ref, kp_ref, vc_ref, vp_ref, o_ref, st_ref, kbuf, vbuf):
    i = pl.program_id(2)
    n_res, tq, _ = q_ref.shape
    kbuf[:, 0:BLOCK, :] = kp_ref[...]
    kbuf[:, BLOCK:BLOCK + tq, :] = kc_ref[...]
    vbuf[:, 0:BLOCK, :] = vp_ref[...]
    vbuf[:, BLOCK:BLOCK + tq, :] = vc_ref[...]
    lo = lax.broadcasted_iota(jnp.int32, (1, LANES), 1) < HEAD_DIM
    first_bias = bias_ref[(i == 0).astype(jnp.int32)]

    for rr in range(n_res):
        for j in range(tq // BLOCK):
            rows = slice(j * BLOCK, (j + 1) * BLOCK)
            keys = slice(j * BLOCK, (j + 2) * BLOCK)
            bias = first_bias if j == 0 else bias_ref[0]
            for p in range(N_PAIRS):
                cols = slice(p * PAIR, (p + 1) * PAIR)
                q2 = q_ref[rr, rows, cols]
                k2 = kbuf[rr, keys, cols]
                v2 = vbuf[rr, keys, cols]
                ms, dens, pvs = [], [], []
                for hh in range(2):
                    qm = jnp.where(lo if hh == 0 else jnp.logical_not(lo), q2, jnp.zeros_like(q2))
                    s = lax.dot_general(qm, k2, (((1,), (1,)), ((), ())),
                                        preferred_element_type=_F32) + bias
                    m = jnp.max(s, axis=-1, keepdims=True)
                    e = jnp.exp2(s - m)
                    ms.append(m)
                    dens.append(jnp.sum(e, axis=-1, keepdims=True))
                    pvs.append(jnp.dot(e.astype(_BF16), v2, preferred_element_type=_F32))
                den = jnp.where(lo, dens[0], dens[1])
                o_ref[rr, rows, cols] = (jnp.where(lo, pvs[0], pvs[1]) * (1.0 / den)).astype(_BF16)
                st_ref[rr, rows, cols] = jnp.where(lo, ms[0], ms[1]) * LN2 + jnp.log(den)


def _band_bias():
    i = jnp.arange(BLOCK)[:, None]
    j = jnp.arange(2 * BLOCK)[None, :]
    dist = BLOCK + i - j
    ok = (dist >= 0) & (dist <= BLOCK)
    ok_first = ok & (j >= BLOCK)
    neg = jnp.float32(-jnp.inf)
    return jnp.stack([jnp.where(ok, 0.0, neg), jnp.where(ok_first, 0.0, neg)]).astype(_F32)


def _attn_branch(bias, q, k, v):
    b, dil, m, c = q.shape
    tq = min(ATTN_ROWS, m)
    n_res = ATTN_ROWS // tq
    g = tq // BLOCK
    cur = lambda bi, r, i: (bi, r, i, 0)
    prev = lambda bi, r, i: (bi, r, jnp.maximum(i * g - 1, 0), 0)
    return pl.pallas_call(
        _attn_kernel,
        grid=(b, dil // n_res, m // tq),
        in_specs=[
            pl.BlockSpec(bias.shape, lambda bi, r, i: (0, 0, 0)),
            pl.BlockSpec((None, n_res, tq, c), cur),
            pl.BlockSpec((None, n_res, tq, c), cur),
            pl.BlockSpec((None, n_res, BLOCK, c), prev),
            pl.BlockSpec((None, n_res, tq, c), cur),
            pl.BlockSpec((None, n_res, BLOCK, c), prev),
        ],
        out_specs=[pl.BlockSpec((None, n_res, tq, c), cur),
                   pl.BlockSpec((None, n_res, tq, c), cur)],
        out_shape=[jax.ShapeDtypeStruct((b, dil, m, c), _BF16),
                   jax.ShapeDtypeStruct((b, dil, m, c), _F32)],
        scratch_shapes=[pltpu.VMEM((n_res, BLOCK + tq, c), _BF16),
                        pltpu.VMEM((n_res, BLOCK + tq, c), _BF16)],
        compiler_params=pltpu.CompilerParams(
            dimension_semantics=("arbitrary", "arbitrary", "arbitrary"),
            vmem_limit_bytes=VMEM_LIMIT),
        name=f"attn_d{dil}",
    )(bias, q, k, k, v, v)


def _out_ffn_kernel(x_ref, mixc_ref, o1_ref, o4_ref, o16_ref, s1_ref, s4_ref, s16_ref,
                    gao_ref, wo_ref, gffn_ref, wg_ref, wu_ref, wd_ref, out_ref, obuf, sbuf):
    tm = x_ref.shape[1]

    def natural(slot, o_ref, s_ref, d):
        if d == 1:
            return o_ref[0, 0].astype(_F32), s_ref[0, 0]
        n = tm // d
        for r in range(d):
            for g in range(N_PAIRS):
                plane, lanes = slot * N_PAIRS + g, slice(g * LANES, (g + 1) * LANES)
                obuf[plane, pl.ds(r, n, stride=d), :] = o_ref[0, r, :, lanes].astype(_F32)
                sbuf[plane, pl.ds(r, n, stride=d), :] = s_ref[0, r, :, lanes]
        planes = range(slot * N_PAIRS, (slot + 1) * N_PAIRS)
        return (jnp.concatenate([obuf[g] for g in planes], axis=-1),
                jnp.concatenate([sbuf[g] for g in planes], axis=-1))

    branches = [natural(0, o1_ref, s1_ref, 1), natural(0, o4_ref, s4_ref, 4),
                natural(1, o16_ref, s16_ref, 16)]
    lses = [st for _, st in branches]
    top = jnp.maximum(jnp.maximum(lses[0], lses[1]), lses[2])
    es = [jnp.exp(l - top) for l in lses]
    y = (es[0] * branches[0][0] + es[1] * branches[1][0] + es[2] * branches[2][0]) * (
        1.0 / (es[0] + es[1] + es[2]))
    ya = _rms(y, gao_ref[...]).astype(_BF16)
    mix = jnp.concatenate([mixc_ref[0], ya], axis=-1)
    x1 = x_ref[0] + jnp.dot(mix, wo_ref[...], preferred_element_type=_F32)

    h = _rms(x1, gffn_ref[...]).astype(_BF16)
    acc = x1
    for c in range(D_FF // FF_CHUNK):
        cols = slice(c * FF_CHUNK, (c + 1) * FF_CHUNK)
        gate = jnp.dot(h, wg_ref[:, cols], preferred_element_type=_F32)
        up = jnp.dot(h, wu_ref[:, cols], preferred_element_type=_F32)
        act = (gate * jax.nn.sigmoid(gate) * up).astype(_BF16)
        acc = acc + jnp.dot(act, wd_ref[cols, :], preferred_element_type=_F32)
    out_ref[0] = acc


def _out_ffn(x, mixc, outs, stats, g_attn_out, w_out, g_ffn, w_gate, w_up, w_down):
    b, s, _ = x.shape
    tm = ROW_TILE
    row = lambda bi, si: (bi, si, 0)
    const = lambda bi, si: (0, 0)
    perm = lambda bi, si: (bi, 0, si, 0)
    resident = lambda a: pl.BlockSpec(a.shape, const, pipeline_mode=pl.Buffered(1))
    in_specs = [pl.BlockSpec((1, tm, D_MODEL), row), pl.BlockSpec((1, tm, D_CONV), row)]
    in_specs += [pl.BlockSpec((1, d, tm // d, D_ATTN), perm) for d in DILATIONS]
    in_specs += [pl.BlockSpec((1, d, tm // d, D_ATTN), perm) for d in DILATIONS]
    in_specs += [pl.BlockSpec((1, D_ATTN), const),
                 resident(w_out), pl.BlockSpec((1, D_MODEL), const),
                 resident(w_gate), resident(w_up), resident(w_down)]
    return pl.pallas_call(
        _out_ffn_kernel,
        grid=(b, s // tm),
        in_specs=in_specs,
        out_specs=pl.BlockSpec((1, tm, D_MODEL), row),
        out_shape=jax.ShapeDtypeStruct(x.shape, _F32),
        scratch_shapes=[pltpu.VMEM((2 * N_PAIRS, tm, LANES), _F32),
                        pltpu.VMEM((2 * N_PAIRS, tm, LANES), _F32)],
        compiler_params=pltpu.CompilerParams(
            dimension_semantics=("arbitrary", "arbitrary"), vmem_limit_bytes=VMEM_LIMIT),
        name="out_ffn",
    )(x, mixc, *outs, *stats, g_attn_out, w_out, g_ffn, w_gate, w_up, w_down)


def kernel(x, g_mix, w_in, conv_w, g_q, g_k, g_conv_out, g_attn_out, w_out, g_ffn, w_gate, w_up, w_down):
    depth = w_in.shape[0]
    bias = _band_bias()
    for l in range(depth):
        gq_t = jnp.tile(g_q[l] * (HEAD_DIM ** -0.5 * LOG2E), N_HEADS)[None, :]
        gk_t = jnp.tile(g_k[l], N_HEADS)[None, :]
        mixc, q1, k1, v1, q4, k4, v4, q16, k16, v16 = _in_proj(
            x, g_mix[l][None, :], w_in[l].astype(_BF16), conv_w[l], gq_t, gk_t,
            g_conv_out[l][None, :])
        outs, stats = [], []
        for q, k, v in ((q1[:, None], k1[:, None], v1[:, None]), (q4, k4, v4), (q16, k16, v16)):
            o, st = _attn_branch(bias, q, k, v)
            outs.append(o)
            stats.append(st)
        x = _out_ffn(x, mixc, outs, stats, g_attn_out[l][None, :], w_out[l].astype(_BF16),
                     g_ffn[l][None, :], w_gate[l].astype(_BF16), w_up[l].astype(_BF16),
                     w_down[l].astype(_BF16))
    return x
```

```python
import jax
import jax.numpy as jnp
from jax import lax
from jax.experimental import pallas as pl
from jax.experimental.pallas import tpu as pltpu

D_MODEL = 1024
HEAD_DIM = 64
D_CONV = 512
D_ATTN = 512
N_HEADS = D_ATTN // HEAD_DIM
DILATIONS = (1, 4, 16)
BLOCK = 128
D_FF = 2816
EPS = 1e-6
LOG2E = 1.4426950408889634
LN2 = 0.6931471805599453

LANES = 128
SUBLANES = 8
PAIR = 2 * HEAD_DIM
N_PAIRS = D_ATTN // PAIR
ROW_TILE = 512
ATTN_ROWS = 2048
FF_CHUNK = 1408
VMEM_LIMIT = 56 * 1024 * 1024

assert PAIR == LANES and D_FF % FF_CHUNK == 0 and FF_CHUNK % LANES == 0

_F32 = jnp.float32
_BF16 = jnp.bfloat16


def _rms(x, g):
    ms = jnp.mean(x * x, axis=-1, keepdims=True)
    return x * lax.rsqrt(ms + EPS) * g


def _head_rms(z, g_tiled):
    lo = lax.broadcasted_iota(jnp.int32, (1, PAIR), 1) < HEAD_DIM
    outs = []
    for p in range(N_PAIRS):
        z2 = z[:, p * PAIR:(p + 1) * PAIR]
        sq = z2 * z2
        s_lo = jnp.sum(jnp.where(lo, sq, 0.0), axis=-1, keepdims=True)
        s_hi = jnp.sum(jnp.where(lo, 0.0, sq), axis=-1, keepdims=True)
        ms = jnp.where(lo, s_lo, s_hi) * (1.0 / HEAD_DIM)
        outs.append(z2 * lax.rsqrt(ms + EPS))
    return jnp.concatenate(outs, axis=-1) * g_tiled


def _in_proj_kernel(x_ref, gmix_ref, w_ref, cw_ref, gq_ref, gk_ref, gco_ref,
                    mixc_ref, q1, k1, v1, q4, k4, v4, q16, k16, v16, cbuf, pbuf, p4buf):
    s_idx = pl.program_id(1)
    tm = x_ref.shape[1]
    h = _rms(x_ref[0], gmix_ref[...]).astype(_BF16)

    def proj(c):
        return jnp.dot(h, w_ref[:, c * D_CONV:(c + 1) * D_CONV], preferred_element_type=_F32)

    @pl.when(s_idx == 0)
    def _():
        cbuf[0:SUBLANES, :] = jnp.zeros((SUBLANES, D_CONV), _F32)

    @pl.when(s_idx > 0)
    def _():
        cbuf[0:SUBLANES, :] = cbuf[tm:tm + SUBLANES, :]

    cu = proj(2) * proj(0)
    cbuf[SUBLANES:SUBLANES + tm, :] = cu
    c1 = cbuf[SUBLANES - 1:SUBLANES - 1 + tm, :]
    c2 = cbuf[SUBLANES - 2:SUBLANES - 2 + tm, :]
    cw = cw_ref[...]
    y = proj(1) * (cw[0:1, :] * c2 + cw[1:2, :] * c1 + cw[2:3, :] * cu)
    mixc_ref[0] = _rms(y, gco_ref[...]).astype(_BF16)

    def emit(slot, z, nat_ref, ref4, ref16):
        nat_ref[0] = z.astype(_BF16)
        for g in range(N_PAIRS):
            pbuf[slot * N_PAIRS + g] = z[:, g * LANES:(g + 1) * LANES]
        for g in range(N_PAIRS):
            plane, lanes = slot * N_PAIRS + g, slice(g * LANES, (g + 1) * LANES)
            for c in range(4):
                blk = pbuf[plane, pl.ds(c, tm // 4, stride=4), :]
                ref4[0, c, :, lanes] = blk.astype(_BF16)
                p4buf[plane, c] = blk
            for c in range(4):
                for a in range(4):
                    ref16[0, 4 * a + c, :, lanes] = (
                        p4buf[plane, c, pl.ds(a, tm // 16, stride=4), :].astype(_BF16))

    emit(0, _head_rms(proj(3), gq_ref[...]), q1, q4, q16)
    emit(1, _head_rms(proj(4), gk_ref[...]), k1, k4, k16)
    emit(2, proj(5), v1, v4, v16)


def _in_proj(x, g_mix, w_in, conv_w, gq_t, gk_t, g_conv_out):
    b, s, _ = x.shape
    tm = ROW_TILE
    row = lambda bi, si: (bi, si, 0)
    const = lambda bi, si: (0, 0)
    nat_sds = jax.ShapeDtypeStruct((b, s, D_CONV), _BF16)
    out_specs = [pl.BlockSpec((1, tm, D_CONV), row)] * 4
    out_shape = [nat_sds] * 4
    for d in DILATIONS[1:]:
        out_specs += [pl.BlockSpec((1, d, tm // d, D_ATTN), lambda bi, si: (bi, 0, si, 0))] * 3
        out_shape += [jax.ShapeDtypeStruct((b, d, s // d, D_ATTN), _BF16)] * 3
    return pl.pallas_call(
        _in_proj_kernel,
        grid=(b, s // tm),
        in_specs=[
            pl.BlockSpec((1, tm, D_MODEL), row),
            pl.BlockSpec((1, D_MODEL), const),
            pl.BlockSpec(w_in.shape, const),
            pl.BlockSpec(conv_w.shape, const),
            pl.BlockSpec((1, D_ATTN), const),
            pl.BlockSpec((1, D_ATTN), const),
            pl.BlockSpec((1, D_CONV), const),
        ],
        out_specs=out_specs,
        out_shape=out_shape,
        scratch_shapes=[pltpu.VMEM((SUBLANES + tm, D_CONV), _F32),
                        pltpu.VMEM((3 * N_PAIRS, tm, LANES), _F32),
                        pltpu.VMEM((3 * N_PAIRS, 4, tm // 4, LANES), _F32)],
        compiler_params=pltpu.CompilerParams(
            dimension_semantics=("arbitrary", "arbitrary"), vmem_limit_bytes=VMEM_LIMIT),
        name="in_proj",
    )(x, g_mix, w_in, conv_w, gq_t, gk_t, g_conv_out)


def _attn_kernel(bias_ref, q_ref, kc_ref, kp_ref, vc_ref, vp_ref, o_ref, st_ref, kbuf, vbuf):
    i = pl.program_id(2)
    n_res, tq, _ = q_ref.shape
    kbuf[:, 0:BLOCK, :] = kp_ref[...]
    kbuf[:, BLOCK:BLOCK + tq, :] = kc_ref[...]
    vbuf[:, 0:BLOCK, :] = vp_ref[...]
    vbuf[:, BLOCK:BLOCK + tq, :] = vc_ref[...]
    lo = lax.broadcasted_iota(jnp.int32, (1, LANES), 1) < HEAD_DIM
    first_bias = bias_ref[(i == 0).astype(jnp.int32)]

    for rr in range(n_res):
        for j in range(tq // BLOCK):
            rows = slice(j * BLOCK, (j + 1) * BLOCK)
            keys = slice(j * BLOCK, (j + 2) * BLOCK)
            bias = first_bias if j == 0 else bias_ref[0]
            for p in range(N_PAIRS):
                cols = slice(p * PAIR, (p + 1) * PAIR)
                q2 = q_ref[rr, rows, cols]
                k2 = kbuf[rr, keys, cols]
                v2 = vbuf[rr, keys, cols]
                ms, dens, pvs = [], [], []
                for hh in range(2):
                    qm = jnp.where(lo if hh == 0 else jnp.logical_not(lo), q2, jnp.zeros_like(q2))
                    s = lax.dot_general(qm, k2, (((1,), (1,)), ((), ())),
                                        preferred_element_type=_F32) + bias
                    m = jnp.max(s, axis=-1, keepdims=True)
                    e = jnp.exp2(s - m)
                    ms.append(m)
                    dens.append(jnp.sum(e, axis=-1, keepdims=True))
                    pvs.append(jnp.dot(e.astype(_BF16), v2, preferred_element_type=_F32))
                den = jnp.where(lo, dens[0], dens[1])
                o_ref[rr, rows, cols] = (jnp.where(lo, pvs[0], pvs[1]) * (1.0 / den)).astype(_BF16)
                st_ref[rr, rows, cols] = jnp.where(lo, ms[0], ms[1]) * LN2 + jnp.log(den)


def _band_bias():
    i = jnp.arange(BLOCK)[:, None]
    j = jnp.arange(2 * BLOCK)[None, :]
    dist = BLOCK + i - j
    ok = (dist >= 0) & (dist <= BLOCK)
    ok_first = ok & (j >= BLOCK)
    neg = jnp.float32(-jnp.inf)
    return jnp.stack([jnp.where(ok, 0.0, neg), jnp.where(ok_first, 0.0, neg)]).astype(_F32)


def _attn_branch(bias, q, k, v):
    b, dil, m, c = q.shape
    tq = min(ATTN_ROWS, m)
    n_res = ATTN_ROWS // tq
    g = tq // BLOCK
    cur = lambda bi, r, i: (bi, r, i, 0)
    prev = lambda bi, r, i: (bi, r, jnp.maximum(i * g - 1, 0), 0)
    return pl.pallas_call(
        _attn_kernel,
        grid=(b, dil // n_res, m // tq),
        in_specs=[
            pl.BlockSpec(bias.shape, lambda bi, r, i: (0, 0, 0)),
            pl.BlockSpec((None, n_res, tq, c), cur),
            pl.BlockSpec((None, n_res, tq, c), cur),
            pl.BlockSpec((None, n_res, BLOCK, c), prev),
            pl.BlockSpec((None, n_res, tq, c), cur),
            pl.BlockSpec((None, n_res, BLOCK, c), prev),
        ],
        out_specs=[pl.BlockSpec((None, n_res, tq, c), cur),
                   pl.BlockSpec((None, n_res, tq, c), cur)],
        out_shape=[jax.ShapeDtypeStruct((b, dil, m, c), _BF16),
                   jax.ShapeDtypeStruct((b, dil, m, c), _F32)],
        scratch_shapes=[pltpu.VMEM((n_res, BLOCK + tq, c), _BF16),
                        pltpu.VMEM((n_res, BLOCK + tq, c), _BF16)],
        compiler_params=pltpu.CompilerParams(
            dimension_semantics=("arbitrary", "arbitrary", "arbitrary"),
            vmem_limit_bytes=VMEM_LIMIT),
        name=f"attn_d{dil}",
    )(bias, q, k, k, v, v)


def _out_ffn_kernel(x_ref, mixc_ref, o1_ref, o4_ref, o16_ref, s1_ref, s4_ref, s16_ref,
                    gao_ref, wo_ref, gffn_ref, wg_ref, wu_ref, wd_ref, out_ref,
                    obuf, sbuf, o4buf, s4buf):
    tm = x_ref.shape[1]

    def natural(slot, o_ref, s_ref, d):
        if d == 1:
            return o_ref[0, 0].astype(_F32), s_ref[0, 0]
        for g in range(N_PAIRS):
            plane, lanes = slot * N_PAIRS + g, slice(g * LANES, (g + 1) * LANES)
            for c in range(4):
                if d == 4:
                    o4, s4 = o_ref[0, c, :, lanes].astype(_F32), s_ref[0, c, :, lanes]
                else:
                    for a in range(4):
                        rows = pl.ds(a, tm // 16, stride=4)
                        o4buf[g, c, rows, :] = o_ref[0, 4 * a + c, :, lanes].astype(_F32)
                        s4buf[g, c, rows, :] = s_ref[0, 4 * a + c, :, lanes]
                    o4, s4 = o4buf[g, c], s4buf[g, c]
                obuf[plane, pl.ds(c, tm // 4, stride=4), :] = o4
                sbuf[plane, pl.ds(c, tm // 4, stride=4), :] = s4
        planes = range(slot * N_PAIRS, (slot + 1) * N_PAIRS)
        return (jnp.concatenate([obuf[g] for g in planes], axis=-1),
                jnp.concatenate([sbuf[g] for g in planes], axis=-1))

    branches = [natural(0, o1_ref, s1_ref, 1), natural(0, o4_ref, s4_ref, 4),
                natural(1, o16_ref, s16_ref, 16)]
    lses = [st for _, st in branches]
    top = jnp.maximum(jnp.maximum(lses[0], lses[1]), lses[2])
    es = [jnp.exp(l - top) for l in lses]
    y = (es[0] * branches[0][0] + es[1] * branches[1][0] + es[2] * branches[2][0]) * (
        1.0 / (es[0] + es[1] + es[2]))
    ya = _rms(y, gao_ref[...]).astype(_BF16)
    mix = jnp.concatenate([mixc_ref[0], ya], axis=-1)
    x1 = x_ref[0] + jnp.dot(mix, wo_ref[...], preferred_element_type=_F32)

    h = _rms(x1, gffn_ref[...]).astype(_BF16)
    acc = x1
    for c in range(D_FF // FF_CHUNK):
        cols = slice(c * FF_CHUNK, (c + 1) * FF_CHUNK)
        gate = jnp.dot(h, wg_ref[:, cols], preferred_element_type=_F32)
        up = jnp.dot(h, wu_ref[:, cols], preferred_element_type=_F32)
        act = (gate * jax.nn.sigmoid(gate) * up).astype(_BF16)
        acc = acc + jnp.dot(act, wd_ref[cols, :], preferred_element_type=_F32)
    out_ref[0] = acc


def _out_ffn(x, mixc, outs, stats, g_attn_out, w_out, g_ffn, w_gate, w_up, w_down):
    b, s, _ = x.shape
    tm = ROW_TILE
    row = lambda bi, si: (bi, si, 0)
    const = lambda bi, si: (0, 0)
    perm = lambda bi, si: (bi, 0, si, 0)
    resident = lambda a: pl.BlockSpec(a.shape, const, pipeline_mode=pl.Buffered(1))
    in_specs = [pl.BlockSpec((1, tm, D_MODEL), row), pl.BlockSpec((1, tm, D_CONV), row)]
    in_specs += [pl.BlockSpec((1, d, tm // d, D_ATTN), perm) for d in DILATIONS]
    in_specs += [pl.BlockSpec((1, d, tm // d, D_ATTN), perm) for d in DILATIONS]
    in_specs += [pl.BlockSpec((1, D_ATTN), const),
                 resident(w_out), pl.BlockSpec((1, D_MODEL), const),
                 resident(w_gate), resident(w_up), resident(w_down)]
    return pl.pallas_call(
        _out_ffn_kernel,
        grid=(b, s // tm),
        in_specs=in_specs,
        out_specs=pl.BlockSpec((1, tm, D_MODEL), row),
        out_shape=jax.ShapeDtypeStruct(x.shape, _F32),
        scratch_shapes=[pltpu.VMEM((2 * N_PAIRS, tm, LANES), _F32),
                        pltpu.VMEM((2 * N_PAIRS, tm, LANES), _F32),
                        pltpu.VMEM((N_PAIRS, 4, tm // 4, LANES), _F32),
                        pltpu.VMEM((N_PAIRS, 4, tm // 4, LANES), _F32)],
        compiler_params=pltpu.CompilerParams(
            dimension_semantics=("arbitrary", "arbitrary"), vmem_limit_bytes=VMEM_LIMIT),
        name="out_ffn",
    )(x, mixc, *outs, *stats, g_attn_out, w_out, g_ffn, w_gate, w_up, w_down)


def kernel(x, g_mix, w_in, conv_w, g_q, g_k, g_conv_out, g_attn_out, w_out, g_ffn, w_gate, w_up, w_down):
    depth = w_in.shape[0]
    bias = _band_bias()
    for l in range(depth):
        gq_t = jnp.tile(g_q[l] * (HEAD_DIM ** -0.5 * LOG2E), N_HEADS)[None, :]
        gk_t = jnp.tile(g_k[l], N_HEADS)[None, :]
        mixc, q1, k1, v1, q4, k4, v4, q16, k16, v16 = _in_proj(
            x, g_mix[l][None, :], w_in[l].astype(_BF16), conv_w[l], gq_t, gk_t,
            g_conv_out[l][None, :])
        outs, stats = [], []
        for q, k, v in ((q1[:, None], k1[:, None], v1[:, None]), (q4, k4, v4), (q16, k16, v16)):
            o, st = _attn_branch(bias, q, k, v)
            outs.append(o)
            stats.append(st)
        x = _out_ffn(x, mixc, outs, stats, g_attn_out[l][None, :], w_out[l].astype(_BF16),
                     g_ffn[l][None, :], w_gate[l].astype(_BF16), w_up[l].astype(_BF16),
                     w_down[l].astype(_BF16))
    return x
```

```python
import jax
import jax.numpy as jnp
from jax import lax
from jax.experimental import pallas as pl
from jax.experimental.pallas import tpu as pltpu

D_MODEL = 1024
HEAD_DIM = 64
D_CONV = 512
D_ATTN = 512
N_HEADS = D_ATTN // HEAD_DIM
DILATIONS = (1, 4, 16)
BLOCK = 128
D_FF = 2816
EPS = 1e-6
LOG2E = 1.4426950408889634
LN2 = 0.6931471805599453

LANES = 128
SUBLANES = 8
PAIR = 2 * HEAD_DIM
N_PAIRS = D_ATTN // PAIR
ROW_TILE = 512
ATTN_ROWS = 2048
FF_CHUNK = 2816
VMEM_LIMIT = 56 * 1024 * 1024

assert PAIR == LANES and D_FF % FF_CHUNK == 0 and FF_CHUNK % LANES == 0

_F32 = jnp.float32
_BF16 = jnp.bfloat16


def _rms(x, g):
    ms = jnp.mean(x * x, axis=-1, keepdims=True)
    return x * lax.rsqrt(ms + EPS) * g


def _head_rms(z, g_tiled):
    lo = lax.broadcasted_iota(jnp.int32, (1, PAIR), 1) < HEAD_DIM
    outs = []
    for p in range(N_PAIRS):
        z2 = z[:, p * PAIR:(p + 1) * PAIR]
        sq = z2 * z2
        s_lo = jnp.sum(jnp.where(lo, sq, 0.0), axis=-1, keepdims=True)
        s_hi = jnp.sum(jnp.where(lo, 0.0, sq), axis=-1, keepdims=True)
        ms = jnp.where(lo, s_lo, s_hi) * (1.0 / HEAD_DIM)
        outs.append(z2 * lax.rsqrt(ms + EPS))
    return jnp.concatenate(outs, axis=-1) * g_tiled


def _in_proj_kernel(x_ref, gmix_ref, w_ref, cw_ref, gq_ref, gk_ref, gco_ref,
                    mixc_ref, q1, k1, v1, q4, k4, v4, q16, k16, v16, cbuf, pbuf, p4buf):
    s_idx = pl.program_id(1)
    tm = x_ref.shape[1]
    h = _rms(x_ref[0], gmix_ref[...]).astype(_BF16)

    def proj(c):
        return jnp.dot(h, w_ref[:, c * D_CONV:(c + 1) * D_CONV], preferred_element_type=_F32)

    @pl.when(s_idx == 0)
    def _():
        cbuf[0:SUBLANES, :] = jnp.zeros((SUBLANES, D_CONV), _F32)

    @pl.when(s_idx > 0)
    def _():
        cbuf[0:SUBLANES, :] = cbuf[tm:tm + SUBLANES, :]

    cu = proj(2) * proj(0)
    cbuf[SUBLANES:SUBLANES + tm, :] = cu
    c1 = cbuf[SUBLANES - 1:SUBLANES - 1 + tm, :]
    c2 = cbuf[SUBLANES - 2:SUBLANES - 2 + tm, :]
    cw = cw_ref[...]
    y = proj(1) * (cw[0:1, :] * c2 + cw[1:2, :] * c1 + cw[2:3, :] * cu)
    mixc_ref[0] = _rms(y, gco_ref[...]).astype(_BF16)

    def emit(slot, z, nat_ref, ref4, ref16):
        nat_ref[0] = z.astype(_BF16)
        for g in range(N_PAIRS):
            pbuf[slot * N_PAIRS + g] = z[:, g * LANES:(g + 1) * LANES]
        for g in range(N_PAIRS):
            plane, lanes = slot * N_PAIRS + g, slice(g * LANES, (g + 1) * LANES)
            for c in range(4):
                blk = pbuf[plane, pl.ds(c, tm // 4, stride=4), :]
                ref4[0, c, :, lanes] = blk.astype(_BF16)
                p4buf[plane, c] = blk
            for c in range(4):
                for a in range(4):
                    ref16[0, 4 * a + c, :, lanes] = (
                        p4buf[plane, c, pl.ds(a, tm // 16, stride=4), :].astype(_BF16))

    emit(0, _head_rms(proj(3), gq_ref[...]), q1, q4, q16)
    emit(1, _head_rms(proj(4), gk_ref[...]), k1, k4, k16)
    emit(2, proj(5), v1, v4, v16)


def _in_proj(x, g_mix, w_in, conv_w, gq_t, gk_t, g_conv_out):
    b, s, _ = x.shape
    tm = ROW_TILE
    row = lambda bi, si: (bi, si, 0)
    const = lambda bi, si: (0, 0)
    nat_sds = jax.ShapeDtypeStruct((b, s, D_CONV), _BF16)
    out_specs = [pl.BlockSpec((1, tm, D_CONV), row)] * 4
    out_shape = [nat_sds] * 4
    for d in DILATIONS[1:]:
        out_specs += [pl.BlockSpec((1, d, tm // d, D_ATTN), lambda bi, si: (bi, 0, si, 0))] * 3
        out_shape += [jax.ShapeDtypeStruct((b, d, s // d, D_ATTN), _BF16)] * 3
    return pl.pallas_call(
        _in_proj_kernel,
        grid=(b, s // tm),
        in_specs=[
            pl.BlockSpec((1, tm, D_MODEL), row),
            pl.BlockSpec((1, D_MODEL), const),
            pl.BlockSpec(w_in.shape, const),
            pl.BlockSpec(conv_w.shape, const),
            pl.BlockSpec((1, D_ATTN), const),
            pl.BlockSpec((1, D_ATTN), const),
            pl.BlockSpec((1, D_CONV), const),
        ],
        out_specs=out_specs,
        out_shape=out_shape,
        scratch_shapes=[pltpu.VMEM((SUBLANES + tm, D_CONV), _F32),
                        pltpu.VMEM((3 * N_PAIRS, tm, LANES), _F32),
                        pltpu.VMEM((3 * N_PAIRS, 4, tm // 4, LANES), _F32)],
        compiler_params=pltpu.CompilerParams(
            dimension_semantics=("arbitrary", "arbitrary"), vmem_limit_bytes=VMEM_LIMIT),
        name="in_proj",
    )(x, g_mix, w_in, conv_w, gq_t, gk_t, g_conv_out)


def _attn_kernel(bias_ref, q_ref, kc_ref, kp_ref, vc_ref, vp_ref, o_ref, st_ref, kbuf, vbuf):
    i = pl.program_id(2)
    n_res, tq, _ = q_ref.shape
    kbuf[:, 0:BLOCK, :] = kp_ref[...]
    kbuf[:, BLOCK:BLOCK + tq, :] = kc_ref[...]
    vbuf[:, 0:BLOCK, :] = vp_ref[...]
    vbuf[:, BLOCK:BLOCK + tq, :] = vc_ref[...]
    lo = lax.broadcasted_iota(jnp.int32, (1, LANES), 1) < HEAD_DIM
    first_bias = bias_ref[(i == 0).astype(jnp.int32)]

    for rr in range(n_res):
        for j in range(tq // BLOCK):
            rows = slice(j * BLOCK, (j + 1) * BLOCK)
            keys = slice(j * BLOCK, (j + 2) * BLOCK)
            bias = first_bias if j == 0 else bias_ref[0]
            for p in range(N_PAIRS):
                cols = slice(p * PAIR, (p + 1) * PAIR)
                q2 = q_ref[rr, rows, cols]
                k2 = kbuf[rr, keys, cols]
                v2 = vbuf[rr, keys, cols]
                ms, dens, pvs = [], [], []
                for hh in range(2):
                    qm = jnp.where(lo if hh == 0 else jnp.logical_not(lo), q2, jnp.zeros_like(q2))
                    s = lax.dot_general(qm, k2, (((1,), (1,)), ((), ())),
                                        preferred_element_type=_F32) + bias
                    m = jnp.max(s, axis=-1, keepdims=True)
                    e = jnp.exp2(s - m)
                    ms.append(m)
                    dens.append(jnp.sum(e, axis=-1, keepdims=True))
                    pvs.append(jnp.dot(e.astype(_BF16), v2, preferred_element_type=_F32))
                den = jnp.where(lo, dens[0], dens[1])
                o_ref[rr, rows, cols] = (jnp.where(lo, pvs[0], pvs[1]) * (1.0 / den)).astype(_BF16)
                st_ref[rr, rows, cols] = jnp.where(lo, ms[0], ms[1]) * LN2 + jnp.log(den)


def _band_bias():
    i = jnp.arange(BLOCK)[:, None]
    j = jnp.arange(2 * BLOCK)[None, :]
    dist = BLOCK + i - j
    ok = (dist >= 0) & (dist <= BLOCK)
    ok_first = ok & (j >= BLOCK)
    neg = jnp.float32(-jnp.inf)
    return jnp.stack([jnp.where(ok, 0.0, neg), jnp.where(ok_first, 0.0, neg)]).astype(_F32)


def _attn_branch(bias, q, k, v):
    b, dil, m, c = q.shape
    tq = min(ATTN_ROWS, m)
    n_res = ATTN_ROWS // tq
    g = tq // BLOCK
    cur = lambda bi, r, i: (bi, r, i, 0)
    prev = lambda bi, r, i: (bi, r, jnp.maximum(i * g - 1, 0), 0)
    return pl.pallas_call(
        _attn_kernel,
        grid=(b, dil // n_res, m // tq),
        in_specs=[
            pl.BlockSpec(bias.shape, lambda bi, r, i: (0, 0, 0)),
            pl.BlockSpec((None, n_res, tq, c), cur),
            pl.BlockSpec((None, n_res, tq, c), cur),
            pl.BlockSpec((None, n_res, BLOCK, c), prev),
            pl.BlockSpec((None, n_res, tq, c), cur),
            pl.BlockSpec((None, n_res, BLOCK, c), prev),
        ],
        out_specs=[pl.BlockSpec((None, n_res, tq, c), cur),
                   pl.BlockSpec((None, n_res, tq, c), cur)],
        out_shape=[jax.ShapeDtypeStruct((b, dil, m, c), _BF16),
                   jax.ShapeDtypeStruct((b, dil, m, c), _F32)],
        scratch_shapes=[pltpu.VMEM((n_res, BLOCK + tq, c), _BF16),
                        pltpu.VMEM((n_res, BLOCK + tq, c), _BF16)],
        compiler_params=pltpu.CompilerParams(
            dimension_semantics=("arbitrary", "arbitrary", "arbitrary"),
            vmem_limit_bytes=VMEM_LIMIT),
        name=f"attn_d{dil}",
    )(bias, q, k, k, v, v)


def _out_ffn_kernel(x_ref, mixc_ref, o1_ref, o4_ref, o16_ref, s1_ref, s4_ref, s16_ref,
                    gao_ref, wo_ref, gffn_ref, wg_ref, wu_ref, wd_ref, out_ref,
                    obuf, sbuf, o4buf, s4buf):
    tm = x_ref.shape[1]

    def natural(slot, o_ref, s_ref, d):
        if d == 1:
            return o_ref[0, 0].astype(_F32), s_ref[0, 0]
        for g in range(N_PAIRS):
            plane, lanes = slot * N_PAIRS + g, slice(g * LANES, (g + 1) * LANES)
            for c in range(4):
                if d == 4:
                    o4, s4 = o_ref[0, c, :, lanes].astype(_F32), s_ref[0, c, :, lanes]
                else:
                    for a in range(4):
                        rows = pl.ds(a, tm // 16, stride=4)
                        o4buf[g, c, rows, :] = o_ref[0, 4 * a + c, :, lanes].astype(_F32)
                        s4buf[g, c, rows, :] = s_ref[0, 4 * a + c, :, lanes]
                    o4, s4 = o4buf[g, c], s4buf[g, c]
                obuf[plane, pl.ds(c, tm // 4, stride=4), :] = o4
                sbuf[plane, pl.ds(c, tm // 4, stride=4), :] = s4
        planes = range(slot * N_PAIRS, (slot + 1) * N_PAIRS)
        return (jnp.concatenate([obuf[g] for g in planes], axis=-1),
                jnp.concatenate([sbuf[g] for g in planes], axis=-1))

    branches = [natural(0, o1_ref, s1_ref, 1), natural(0, o4_ref, s4_ref, 4),
                natural(1, o16_ref, s16_ref, 16)]
    lses = [st for _, st in branches]
    top = jnp.maximum(jnp.maximum(lses[0], lses[1]), lses[2])
    es = [jnp.exp(l - top) for l in lses]
    y = (es[0] * branches[0][0] + es[1] * branches[1][0] + es[2] * branches[2][0]) * (
        1.0 / (es[0] + es[1] + es[2]))
    ya = _rms(y, gao_ref[...]).astype(_BF16)
    mix = jnp.concatenate([mixc_ref[0], ya], axis=-1)
    x1 = x_ref[0] + jnp.dot(mix, wo_ref[...], preferred_element_type=_F32)

    h = _rms(x1, gffn_ref[...]).astype(_BF16)
    acc = x1
    for c in range(D_FF // FF_CHUNK):
        cols = slice(c * FF_CHUNK, (c + 1) * FF_CHUNK)
        gate = jnp.dot(h, wg_ref[:, cols], preferred_element_type=_F32)
        up = jnp.dot(h, wu_ref[:, cols], preferred_element_type=_F32)
        act = (gate * jax.nn.sigmoid(gate) * up).astype(_BF16)
        acc = acc + jnp.dot(act, wd_ref[cols, :], preferred_element_type=_F32)
    out_ref[0] = acc


def _out_ffn(x, mixc, outs, stats, g_attn_out, w_out, g_ffn, w_gate, w_up, w_down):
    b, s, _ = x.shape
    tm = ROW_TILE
    row = lambda bi, si: (bi, si, 0)
    const = lambda bi, si: (0, 0)
    perm = lambda bi, si: (bi, 0, si, 0)
    resident = lambda a: pl.BlockSpec(a.shape, const, pipeline_mode=pl.Buffered(1))
    in_specs = [pl.BlockSpec((1, tm, D_MODEL), row), pl.BlockSpec((1, tm, D_CONV), row)]
    in_specs += [pl.BlockSpec((1, d, tm // d, D_ATTN), perm) for d in DILATIONS]
    in_specs += [pl.BlockSpec((1, d, tm // d, D_ATTN), perm) for d in DILATIONS]
    in_specs += [pl.BlockSpec((1, D_ATTN), const),
                 resident(w_out), pl.BlockSpec((1, D_MODEL), const),
                 resident(w_gate), resident(w_up), resident(w_down)]
    return pl.pallas_call(
        _out_ffn_kernel,
        grid=(b, s // tm),
        in_specs=in_specs,
        out_specs=pl.BlockSpec((1, tm, D_MODEL), row),
        out_shape=jax.ShapeDtypeStruct(x.shape, _F32),
        scratch_shapes=[pltpu.VMEM((2 * N_PAIRS, tm, LANES), _F32),
                        pltpu.VMEM((2 * N_PAIRS, tm, LANES), _F32),
                        pltpu.VMEM((N_PAIRS, 4, tm // 4, LANES), _F32),
                        pltpu.VMEM((N_PAIRS, 4, tm // 4, LANES), _F32)],
        compiler_params=pltpu.CompilerParams(
            dimension_semantics=("arbitrary", "arbitrary"), vmem_limit_bytes=VMEM_LIMIT),
        name="out_ffn",
    )(x, mixc, *outs, *stats, g_attn_out, w_out, g_ffn, w_gate, w_up, w_down)


def kernel(x, g_mix, w_in, conv_w, g_q, g_k, g_conv_out, g_attn_out, w_out, g_ffn, w_gate, w_up, w_down):
    depth = w_in.shape[0]
    bias = _band_bias()
    for l in range(depth):
        gq_t = jnp.tile(g_q[l] * (HEAD_DIM ** -0.5 * LOG2E), N_HEADS)[None, :]
        gk_t = jnp.tile(g_k[l], N_HEADS)[None, :]
        mixc, q1, k1, v1, q4, k4, v4, q16, k16, v16 = _in_proj(
            x, g_mix[l][None, :], w_in[l].astype(_BF16), conv_w[l], gq_t, gk_t,
            g_conv_out[l][None, :])
        outs, stats = [], []
        for q, k, v in ((q1[:, None], k1[:, None], v1[:, None]), (q4, k4, v4), (q16, k16, v16)):
            o, st = _attn_branch(bias, q, k, v)
            outs.append(o)
            stats.append(st)
        x = _out_ffn(x, mixc, outs, stats, g_attn_out[l][None, :], w_out[l].astype(_BF16),
                     g_ffn[l][None, :], w_gate[l].astype(_BF16), w_up[l].astype(_BF16),
                     w_down[l].astype(_BF16))
    return x
```
